```python
import jax, jax.numpy as jnp
from jax import lax
import numpy as np

D_MODEL = 2048
BATCH = 2
SEQ = 8192
DEPTH = 4

D_MIX = D_MODEL
W_POOL = D_MIX // 4
W_CONV = D_MIX // 4
W_SGU = D_MIX // 4
W_SSM = D_MIX - W_POOL - W_CONV - W_SGU
POOL_WINDOWS = (2, 4, 8, 16)
POOL_GROUP = W_POOL // len(POOL_WINDOWS)
CONV_WIDTH = 31
SGU_CHUNK = 128
SGU_HEADS = 4
SGU_HEAD_DIM = W_SGU // SGU_HEADS
SSM_HEAD_DIM = 64
SSM_HEADS = W_SSM // SSM_HEAD_DIM
SSM_GROUPS = 2
SSM_STATE = 128
SSM_CONV = 4
SSM_CHUNK = 128
SSM_CONV_DIM = W_SSM + 2 * SSM_GROUPS * SSM_STATE
D_FF = ((8 * D_MODEL // 3 + 255) // 256) * 256
IN_SIZES = (W_POOL, W_CONV, W_CONV, W_SGU, W_SGU, W_SSM, SSM_CONV_DIM, SSM_HEADS)
IN_COLS = sum(IN_SIZES)
EPS = 1e-6

kernel_name = "hybrid_parallel_pool_conv_sgu_ssd_macaron"


def rmsnorm(x, g):
    xf = x.astype(jnp.float32)
    y = xf * lax.rsqrt(jnp.mean(xf * xf, axis=-1, keepdims=True) + EPS)
    return (y * g.astype(jnp.float32)).astype(x.dtype)


def layernorm(x, g, b):
    xf = x.astype(jnp.float32)
    mu = jnp.mean(xf, axis=-1, keepdims=True)
    var = jnp.mean(jnp.square(xf - mu), axis=-1, keepdims=True)
    y = (xf - mu) * lax.rsqrt(var + EPS)
    return (y * g.astype(jnp.float32) + b.astype(jnp.float32)).astype(x.dtype)


def swiglu(h, w_gate, w_up, w_down):
    return (jax.nn.silu(h @ w_gate) * (h @ w_up)) @ w_down


def causal_depthwise_conv(x, w, b):
    k, c = w.shape
    y = lax.conv_general_dilated(
        x, w[:, None, :].astype(x.dtype), window_strides=(1,), padding=[(k - 1, 0)],
        dimension_numbers=("NWC", "WIO", "NWC"), feature_group_count=c)
    return y + b


def pool_mixer(a, w, scale):
    bsz, s, _ = a.shape
    af = a.astype(jnp.float32).reshape(bsz, s, len(POOL_WINDOWS), POOL_GROUP)
    cs = jnp.cumsum(af, axis=1)
    outs = []
    for g, win in enumerate(POOL_WINDOWS):
        c = cs[:, :, g]
        shifted = jnp.pad(c, ((0, 0), (win, 0), (0, 0)))[:, :s]
        count = jnp.minimum(jnp.arange(1, s + 1), win).astype(jnp.float32)[None, :, None]
        outs.append((c - shifted) / count - af[:, :, g])
    p = jnp.stack(outs, axis=2).astype(a.dtype)
    y = jnp.einsum("bsgc,gcd->bsgd", p, w).reshape(bsz, s, W_POOL)
    return y * scale


def conv_module(val, gate, dw_w, dw_b, ln_g, ln_b, pw_w, pw_b):
    h = val * jax.nn.sigmoid(gate)
    h = causal_depthwise_conv(h, dw_w, dw_b)
    h = jax.nn.silu(layernorm(h, ln_g, ln_b))
    return h @ pw_w + pw_b


def sgu_mixer(u, v, ln_g, ln_b, w_s, b_s):
    bsz, s, _ = u.shape
    u = jax.nn.gelu(u)
    v = layernorm(jax.nn.gelu(v), ln_g, ln_b)
    nc = s // SGU_CHUNK
    vc = v.reshape(bsz, nc, SGU_CHUNK, SGU_HEADS, SGU_HEAD_DIM)
    mask = jnp.tril(jnp.ones((SGU_CHUNK, SGU_CHUNK), dtype=bool))
    ws = jnp.where(mask, w_s, jnp.zeros_like(w_s))
    g = jnp.einsum("hts,bcshd->bcthd", ws, vc) + b_s.T[None, None, :, :, None]
    return u * g.reshape(bsz, s, W_SGU)


def ssd_mixer(z, xbc, dt_raw, conv_w, conv_b, dt_bias, a_log, d_skip, norm_g):
    bsz, s, _ = z.shape
    H, P, N, Q = SSM_HEADS, SSM_HEAD_DIM, SSM_STATE, SSM_CHUNK
    rep = SSM_HEADS // SSM_GROUPS
    xbc = jax.nn.silu(causal_depthwise_conv(xbc, conv_w, conv_b))
    xs, bm, cm = jnp.split(xbc, [W_SSM, W_SSM + SSM_GROUPS * N], axis=-1)
    xs = xs.astype(jnp.float32).reshape(bsz, s, H, P)
    bm = jnp.repeat(bm.astype(jnp.float32).reshape(bsz, s, SSM_GROUPS, N), rep, axis=2)
    cm = jnp.repeat(cm.astype(jnp.float32).reshape(bsz, s, SSM_GROUPS, N), rep, axis=2)
    dt = jax.nn.softplus(dt_raw.astype(jnp.float32) + dt_bias.astype(jnp.float32))
    a = -jnp.exp(a_log.astype(jnp.float32))
    nc = s // Q
    x_c = (xs * dt[..., None]).reshape(bsz, nc, Q, H, P)
    b_c = bm.reshape(bsz, nc, Q, H, N)
    c_c = cm.reshape(bsz, nc, Q, H, N)
    a_cs = jnp.cumsum((dt * a).reshape(bsz, nc, Q, H).transpose(0, 3, 1, 2), axis=-1)
    mask = jnp.tril(jnp.ones((Q, Q), dtype=bool))
    seg = a_cs[..., :, None] - a_cs[..., None, :]
    decay = jnp.where(mask, jnp.exp(jnp.where(mask, seg, 0.0)), 0.0)
    scores = jnp.einsum("bclhn,bcshn->bhcls", c_c, b_c) * decay
    y_diag = jnp.einsum("bhcls,bcshp->bclhp", scores, x_c)
    decay_states = jnp.exp(a_cs[..., -1:] - a_cs)
    states = jnp.einsum("bclhn,bhcl,bclhp->bchpn", b_c, decay_states, x_c)
    chunk_decay = jnp.exp(a_cs[..., -1])

    def step(hstate, inp):
        s_c, dec_c = inp
        return hstate * dec_c[..., None, None] + s_c, hstate

    h0 = jnp.zeros((bsz, H, P, N), jnp.float32)
    _, prev = lax.scan(step, h0, (states.transpose(1, 0, 2, 3, 4), chunk_decay.transpose(2, 0, 1)))
    prev = prev.transpose(1, 0, 2, 3, 4)
    y_off = jnp.einsum("bclhn,bchpn,bhcl->bclhp", c_c, prev, jnp.exp(a_cs))
    y = (y_diag + y_off).reshape(bsz, s, H, P) + xs * d_skip.astype(jnp.float32)[:, None]
    y = y.reshape(bsz, s, W_SSM) * jax.nn.silu(z.astype(jnp.float32))
    yg = y.reshape(bsz, s, SSM_GROUPS, W_SSM // SSM_GROUPS)
    yg = yg * lax.rsqrt(jnp.mean(yg * yg, axis=-1, keepdims=True) + EPS)
    return (yg.reshape(bsz, s, W_SSM) * norm_g.astype(jnp.float32)).astype(z.dtype)


def setup_inputs(seed: int = 0) -> dict:
    key = jax.random.key(seed)
    ks = iter(jax.random.split(key, 40))
    f32 = jnp.float32

    def dense(shape, fan_in):
        return jax.random.normal(next(ks), shape, f32) * (fan_in ** -0.5)

    def gain(shape):
        return 1.0 + 0.02 * jax.random.normal(next(ks), shape, f32)

    def bias(shape):
        return 0.02 * jax.random.normal(next(ks), shape, f32)

    L = DEPTH
    x = jax.random.normal(next(ks), (BATCH, SEQ, D_MODEL), f32)
    ffn1_norm = gain((L, D_MODEL))
    ffn1_w_gate = dense((L, D_MODEL, D_FF), D_MODEL)
    ffn1_w_up = dense((L, D_MODEL, D_FF), D_MODEL)
    ffn1_w_down = dense((L, D_FF, D_MODEL), D_FF)
    mix_norm = gain((L, D_MODEL))
    w_in = dense((L, D_MODEL, IN_COLS), D_MODEL)
    pool_w = dense((L, len(POOL_WINDOWS), POOL_GROUP, POOL_GROUP), POOL_GROUP)
    pool_scale = 1.0 + 0.1 * jax.random.normal(next(ks), (L, W_POOL), f32)
    conv_dw_w = dense((L, CONV_WIDTH, W_CONV), CONV_WIDTH)
    conv_dw_b = bias((L, W_CONV))
    conv_ln_g = gain((L, W_CONV))
    conv_ln_b = bias((L, W_CONV))
    conv_pw_w = dense((L, W_CONV, W_CONV), W_CONV)
    conv_pw_b = bias((L, W_CONV))
    sgu_ln_g = gain((L, W_SGU))
    sgu_ln_b = bias((L, W_SGU))
    sgu_w_s = dense((L, SGU_HEADS, SGU_CHUNK, SGU_CHUNK), SGU_CHUNK)
    sgu_b = gain((L, SGU_HEADS, SGU_CHUNK))
    ssm_conv_w = dense((L, SSM_CONV, SSM_CONV_DIM), SSM_CONV)
    ssm_conv_b = bias((L, SSM_CONV_DIM))
    u = jax.random.uniform(next(ks), (L, SSM_HEADS), f32)
    dt0 = jnp.maximum(jnp.exp(u * (jnp.log(0.1) - jnp.log(0.001)) + jnp.log(0.001)), 1e-4)
    ssm_dt_bias = dt0 + jnp.log(-jnp.expm1(-dt0))
    ssm_a_log = jnp.log(jax.random.uniform(next(ks), (L, SSM_HEADS), f32, 1.0, 16.0))
    ssm_d = gain((L, SSM_HEADS))
    ssm_norm = gain((L, W_SSM))
    w_out = dense((L, D_MIX, D_MODEL), D_MIX)
    ffn2_norm = gain((L, D_MODEL))
    ffn2_w_gate = dense((L, D_MODEL, D_FF), D_MODEL)
    ffn2_w_up = dense((L, D_MODEL, D_FF), D_MODEL)
    ffn2_w_down = dense((L, D_FF, D_MODEL), D_FF)
    final_norm = gain((D_MODEL,))
    return {
        "x": x,
        "ffn1_norm": ffn1_norm, "ffn1_w_gate": ffn1_w_gate, "ffn1_w_up": ffn1_w_up, "ffn1_w_down": ffn1_w_down,
        "mix_norm": mix_norm, "w_in": w_in,
        "pool_w": pool_w, "pool_scale": pool_scale,
        "conv_dw_w": conv_dw_w, "conv_dw_b": conv_dw_b, "conv_ln_g": conv_ln_g, "conv_ln_b": conv_ln_b,
        "conv_pw_w": conv_pw_w, "conv_pw_b": conv_pw_b,
        "sgu_ln_g": sgu_ln_g, "sgu_ln_b": sgu_ln_b, "sgu_w_s": sgu_w_s, "sgu_b": sgu_b,
        "ssm_conv_w": ssm_conv_w, "ssm_conv_b": ssm_conv_b, "ssm_dt_bias": ssm_dt_bias,
        "ssm_a_log": ssm_a_log, "ssm_d": ssm_d, "ssm_norm": ssm_norm,
        "w_out": w_out,
        "ffn2_norm": ffn2_norm, "ffn2_w_gate": ffn2_w_gate, "ffn2_w_up": ffn2_w_up, "ffn2_w_down": ffn2_w_down,
        "final_norm": final_norm,
    }


def reference(x, ffn1_norm, ffn1_w_gate, ffn1_w_up, ffn1_w_down, mix_norm, w_in,
              pool_w, pool_scale, conv_dw_w, conv_dw_b, conv_ln_g, conv_ln_b, conv_pw_w, conv_pw_b,
              sgu_ln_g, sgu_ln_b, sgu_w_s, sgu_b, ssm_conv_w, ssm_conv_b, ssm_dt_bias, ssm_a_log,
              ssm_d, ssm_norm, w_out, ffn2_norm, ffn2_w_gate, ffn2_w_up, ffn2_w_down, final_norm):
    split_points = list(np.cumsum(IN_SIZES)[:-1])
    for l in range(DEPTH):
        h = rmsnorm(x, ffn1_norm[l])
        x = x + 0.5 * swiglu(h, ffn1_w_gate[l], ffn1_w_up[l], ffn1_w_down[l])
        h = rmsnorm(x, mix_norm[l])
        proj = h @ w_in[l]
        a, c_val, c_gate, s_u, s_v, m_z, m_xbc, m_dt = jnp.split(proj, split_points, axis=-1)
        y_a = pool_mixer(a, pool_w[l], pool_scale[l])
        y_b = conv_module(c_val, c_gate, conv_dw_w[l], conv_dw_b[l], conv_ln_g[l], conv_ln_b[l],
                          conv_pw_w[l], conv_pw_b[l])
        y_c = sgu_mixer(s_u, s_v, sgu_ln_g[l], sgu_ln_b[l], sgu_w_s[l], sgu_b[l])
        y_d = ssd_mixer(m_z, m_xbc, m_dt, ssm_conv_w[l], ssm_conv_b[l], ssm_dt_bias[l],
                        ssm_a_log[l], ssm_d[l], ssm_norm[l])
        y = jnp.concatenate([y_a, y_b.astype(y_a.dtype), y_c.astype(y_a.dtype), y_d.astype(y_a.dtype)], axis=-1)
        x = x + y @ w_out[l]
        h = rmsnorm(x, ffn2_norm[l])
        x = x + 0.5 * swiglu(h, ffn2_w_gate[l], ffn2_w_up[l], ffn2_w_down[l])
    return rmsnorm(x, final_norm)
```

```python
import functools

import jax
import jax.numpy as jnp
from jax import lax
from jax.experimental import pallas as pl
from jax.experimental.pallas import tpu as pltpu

F32 = jnp.float32
BF16 = jnp.bfloat16

D_MODEL = 2048
DEPTH = 4
D_FF = 5632
EPS = 1e-6

W_GRP = 512
POOL_WINDOWS = (2, 4, 8, 16)
POOL_GROUP = 128
CONV_WIDTH = 31
CHUNK = 128
SGU_HEADS = 4
SSM_HEADS = 8
SSM_HEAD_DIM = 64
SSM_GROUPS = 2
SSM_STATE = 128
SSM_CONV = 4
SSM_CONV_DIM = 1024
N_MAIN = 4096
PROJ_COLS = N_MAIN + W_GRP
HALO = 32

C_A, C_VAL, C_GATE, C_U, C_V, C_Z, C_XBC, C_DT = 0, 512, 1024, 1536, 2048, 2560, 3072, 4096

VMEM_LIMIT = 56 * 1024 * 1024


def _params(sem):
    return pltpu.CompilerParams(dimension_semantics=sem, vmem_limit_bytes=VMEM_LIMIT)


def _rms_rows(x, g):
    ms = jnp.mean(x * x, axis=-1, keepdims=True)
    return x * lax.rsqrt(ms + EPS) * g


def _silu(x):
    return x * jax.nn.sigmoid(x)


def _gelu_tanh(x):
    c = 0.7978845608028654
    return 0.5 * x * (1.0 + jnp.tanh(c * (x + 0.044715 * (x * x * x))))


def _layernorm_rows(x, g, b):
    mu = jnp.mean(x, axis=-1, keepdims=True)
    xc = x - mu
    var = jnp.mean(xc * xc, axis=-1, keepdims=True)
    return xc * lax.rsqrt(var + EPS) * g + b


def _ffn_kernel(x_ref, g_ref, wg_ref, wu_ref, wd_ref, o_ref, h_ref, *, n_f):
    j = pl.program_id(1)

    @pl.when(j == 0)
    def _():
        h_ref[...] = _rms_rows(x_ref[...], g_ref[...]).astype(BF16)
        o_ref[...] = jnp.zeros_like(o_ref)

    h = h_ref[...]
    gate = jnp.dot(h, wg_ref[...], preferred_element_type=F32)
    up = jnp.dot(h, wu_ref[...], preferred_element_type=F32)
    act = (_silu(gate) * up).astype(BF16)
    o_ref[...] += jnp.dot(act, wd_ref[...], preferred_element_type=F32)

    @pl.when(j == n_f - 1)
    def _():
        o_ref[...] = x_ref[...] + 0.5 * o_ref[...]


def _ffn(x, g, wg, wu, wd, *, tm=512, tf=512):
    m = x.shape[0]
    n_f = D_FF // tf
    return pl.pallas_call(
        functools.partial(_ffn_kernel, n_f=n_f),
        grid=(m // tm, n_f),
        in_specs=[
            pl.BlockSpec((tm, D_MODEL), lambda i, j: (i, 0)),
            pl.BlockSpec((1, D_MODEL), lambda i, j: (0, 0)),
            pl.BlockSpec((D_MODEL, tf), lambda i, j: (0, j)),
            pl.BlockSpec((D_MODEL, tf), lambda i, j: (0, j)),
            pl.BlockSpec((tf, D_MODEL), lambda i, j: (j, 0)),
        ],
        out_specs=pl.BlockSpec((tm, D_MODEL), lambda i, j: (i, 0)),
        out_shape=jax.ShapeDtypeStruct((m, D_MODEL), F32),
        scratch_shapes=[pltpu.VMEM((tm, D_MODEL), BF16)],
        compiler_params=_params(("parallel", "arbitrary")),
        name="ffn",
    )(x, g, wg, wu, wd)


def _proj_kernel(x_ref, g_ref, w_ref, o_ref, h_ref):
    @pl.when(pl.program_id(1) == 0)
    def _():
        h_ref[...] = _rms_rows(x_ref[...], g_ref[...]).astype(BF16)

    o_ref[...] = jnp.dot(h_ref[...], w_ref[...], preferred_element_type=F32)


def _proj(x, g, w, *, tm=512, tn=1536):
    m = x.shape[0]
    n = w.shape[1]
    return pl.pallas_call(
        _proj_kernel,
        grid=(m // tm, n // tn),
        in_specs=[
            pl.BlockSpec((tm, D_MODEL), lambda i, j: (i, 0)),
            pl.BlockSpec((1, D_MODEL), lambda i, j: (0, 0)),
            pl.BlockSpec((D_MODEL, tn), lambda i, j: (0, j)),
        ],
        out_specs=pl.BlockSpec((tm, tn), lambda i, j: (i, j)),
        out_shape=jax.ShapeDtypeStruct((m, n), F32),
        scratch_shapes=[pltpu.VMEM((tm, D_MODEL), BF16)],
        compiler_params=_params(("parallel", "arbitrary")),
        name="proj_in",
    )(x, g, w)


def _outproj_kernel(y_ref, w_ref, x_ref, o_ref):
    o_ref[...] = x_ref[...] + jnp.dot(y_ref[...], w_ref[...], preferred_element_type=F32)


def _outproj(y, w, x, *, tm=512):
    m = x.shape[0]
    return pl.pallas_call(
        _outproj_kernel,
        grid=(m // tm,),
        in_specs=[
            pl.BlockSpec((tm, D_MODEL), lambda i: (i, 0)),
            pl.BlockSpec((D_MODEL, D_MODEL), lambda i: (0, 0)),
            pl.BlockSpec((tm, D_MODEL), lambda i: (i, 0)),
        ],
        out_specs=pl.BlockSpec((tm, D_MODEL), lambda i: (i, 0)),
        out_shape=jax.ShapeDtypeStruct((m, D_MODEL), F32),
        compiler_params=_params(("parallel",)),
        name="proj_out",
    )(y, w, x)


def _final_norm_kernel(x_ref, g_ref, o_ref):
    o_ref[...] = _rms_rows(x_ref[...], g_ref[...])


def _final_norm(x, g, *, tm=512):
    m = x.shape[0]
    return pl.pallas_call(
        _final_norm_kernel,
        grid=(m // tm,),
        in_specs=[
            pl.BlockSpec((tm, D_MODEL), lambda i: (i, 0)),
            pl.BlockSpec((1, D_MODEL), lambda i: (0, 0)),
        ],
        out_specs=pl.BlockSpec((tm, D_MODEL), lambda i: (i, 0)),
        out_shape=jax.ShapeDtypeStruct((m, D_MODEL), F32),
        compiler_params=_params(("parallel",)),
        name="final_norm",
    )(x, g)


def _mixer_kernel(proj_ref, pool_w_ref, pool_scale_ref,
                  dw_w_ref, dw_b_ref, cln_g_ref, cln_b_ref, pw_w_ref, pw_b_ref,
                  sln_g_ref, sln_b_ref, sgu_w_ref, sgu_bias_ref,
                  sconv_w_ref, sconv_b_ref, dt_bias_ref, a_log_ref, d_skip_ref, norm_g_ref,
                  head_sel_ref,
                  y_ref,
                  ext_a, ext_h, ext_xbc, state_ref, *, ts):
    t = pl.program_id(1)
    n_chunks = ts // CHUNK

    @pl.when(t == 0)
    def _():
        ext_a[0:HALO, :] = jnp.zeros((HALO, W_GRP), F32)
        ext_h[0:HALO, :] = jnp.zeros((HALO, W_GRP), F32)
        ext_xbc[0:HALO, :] = jnp.zeros((HALO, SSM_CONV_DIM), F32)
        state_ref[...] = jnp.zeros_like(state_ref)

    @pl.when(t > 0)
    def _():
        ext_a[0:HALO, :] = ext_a[ts:ts + HALO, :]
        ext_h[0:HALO, :] = ext_h[ts:ts + HALO, :]
        ext_xbc[0:HALO, :] = ext_xbc[ts:ts + HALO, :]

    for c in range(n_chunks):
        rows = slice(c * CHUNK, (c + 1) * CHUNK)
        dst = slice(HALO + c * CHUNK, HALO + (c + 1) * CHUNK)
        ext_a[dst, :] = proj_ref[0, rows, C_A:C_A + W_GRP]
        ext_h[dst, :] = (proj_ref[0, rows, C_VAL:C_VAL + W_GRP]
                         * jax.nn.sigmoid(proj_ref[0, rows, C_GATE:C_GATE + W_GRP]))
        ext_xbc[dst, :] = proj_ref[0, rows, C_XBC:C_XBC + SSM_CONV_DIM]

    row_i = lax.broadcasted_iota(jnp.int32, (CHUNK, CHUNK), 0)
    col_i = lax.broadcasted_iota(jnp.int32, (CHUNK, CHUNK), 1)
    causal = row_i >= col_i
    tril_ones = jnp.where(causal, 1.0, 0.0).astype(F32)
    lane_lo = col_i < SSM_HEAD_DIM
    sgu_w = [jnp.where(causal, sgu_w_ref[h], 0.0).astype(BF16) for h in range(SGU_HEADS)]
    a_neg = -jnp.exp(a_log_ref[...])

    for c in range(n_chunks):
        rows = slice(c * CHUNK, (c + 1) * CHUNK)
        base = HALO + c * CHUNK

        pos1 = row_i + (t * ts + c * CHUNK + 1)
        for g, win in enumerate(POOL_WINDOWS):
            cols = slice(g * POOL_GROUP, (g + 1) * POOL_GROUP)
            cur = ext_a[base:base + CHUNK, cols]
            s = cur
            for k in range(1, win):
                s = s + ext_a[base - k:base - k + CHUNK, cols]
            cnt = jnp.minimum(pos1, win).astype(F32)
            p = s / cnt - cur
            ya = jnp.dot(p.astype(BF16), pool_w_ref[g], preferred_element_type=F32)
            y_ref[0, rows, g * POOL_GROUP:(g + 1) * POOL_GROUP] = (
                ya * pool_scale_ref[:, cols]).astype(BF16)

        blocks = []
        for cb in range(W_GRP // 128):
            cols = slice(cb * 128, (cb + 1) * 128)
            acc = ext_h[base:base + CHUNK, cols] * dw_w_ref[CONV_WIDTH - 1:CONV_WIDTH, cols]
            for j in range(CONV_WIDTH - 1):
                sh = CONV_WIDTH - 1 - j
                acc = acc + ext_h[base - sh:base - sh + CHUNK, cols] * dw_w_ref[j:j + 1, cols]
            blocks.append(acc)
        hc = jnp.concatenate(blocks, axis=1) + dw_b_ref[...]
        hs = _silu(_layernorm_rows(hc, cln_g_ref[...], cln_b_ref[...]))
        yb = jnp.dot(hs.astype(BF16), pw_w_ref[...], preferred_element_type=F32) + pw_b_ref[...]
        y_ref[0, rows, W_GRP:2 * W_GRP] = yb.astype(BF16)

        u = _gelu_tanh(proj_ref[0, rows, C_U:C_U + W_GRP])
        v = _layernorm_rows(_gelu_tanh(proj_ref[0, rows, C_V:C_V + W_GRP]),
                            sln_g_ref[...], sln_b_ref[...]).astype(BF16)
        for h in range(SGU_HEADS):
            cols = slice(h * 128, (h + 1) * 128)
            gt = jnp.dot(sgu_w[h], v[:, cols], preferred_element_type=F32) + sgu_bias_ref[:, cols]
            y_ref[0, rows, 2 * W_GRP + h * 128:2 * W_GRP + (h + 1) * 128] = (
                u[:, cols] * gt).astype(BF16)

        xbc = ext_xbc[base:base + CHUNK, :] * sconv_w_ref[SSM_CONV - 1:SSM_CONV, :]
        for j in range(SSM_CONV - 1):
            sh = SSM_CONV - 1 - j
            xbc = xbc + ext_xbc[base - sh:base - sh + CHUNK, :] * sconv_w_ref[j:j + 1, :]
        xbc = _silu(xbc + sconv_b_ref[...])
        xs = xbc[:, 0:W_GRP]
        dt_pre = proj_ref[0, rows, C_DT:C_DT + W_GRP] + dt_bias_ref[...]
        dt = jnp.maximum(dt_pre, 0.0) + jnp.log1p(jnp.exp(-jnp.abs(dt_pre)))
        x_c = xs * dt
        a_cs = jnp.dot(tril_ones, dt * a_neg, preferred_element_type=F32,
                       precision=lax.Precision.HIGHEST)
        a_last = a_cs[CHUNK - 1:CHUNK, :]
        exp_acs = jnp.exp(a_cs)
        x_dec = (x_c * jnp.exp(a_last - a_cs)).astype(BF16)
        chunk_decay = jnp.exp(a_last)
        a_cs_t = lax.dot_general(head_sel_ref[...], a_cs, (((1,), (1,)), ((), ())),
                                 preferred_element_type=F32, precision=lax.Precision.HIGHEST)

        y_parts = []
        for g in range(SSM_GROUPS):
            bm = xbc[:, W_GRP + g * SSM_STATE:W_GRP + (g + 1) * SSM_STATE].astype(BF16)
            cm = xbc[:, W_GRP + (SSM_GROUPS + g) * SSM_STATE:
                     W_GRP + (SSM_GROUPS + g + 1) * SSM_STATE].astype(BF16)
            scores = lax.dot_general(cm, bm, (((1,), (1,)), ((), ())), preferred_element_type=F32)
            gcols = slice(g * 256, (g + 1) * 256)
            for jb in range(2):
                blk = 2 * g + jb
                cols = slice(blk * 128, (blk + 1) * 128)
                acs_blk = a_cs[:, cols]
                swapped = pltpu.roll(acs_blk, SSM_HEAD_DIM, axis=1)
                xcb = x_c[:, cols]
                yd = None
                for half in range(2):
                    head = 2 * blk + half
                    col_b = jnp.where(lane_lo, acs_blk, swapped) if half == 0 else \
                        jnp.where(lane_lo, swapped, acs_blk)
                    seg = col_b - a_cs_t[head:head + 1, :]
                    decay = jnp.where(causal, jnp.exp(seg), 0.0)
                    keep = lane_lo if half == 0 else jnp.logical_not(lane_lo)
                    xh = jnp.where(keep, xcb, 0.0).astype(BF16)
                    part = jnp.dot((scores * decay).astype(BF16), xh, preferred_element_type=F32)
                    yd = part if yd is None else yd + part
                y_parts.append(yd)
            st = state_ref[:, gcols]
            y_off = jnp.dot(cm, st.astype(BF16), preferred_element_type=F32) * exp_acs[:, gcols]
            y_parts[2 * g] = y_parts[2 * g] + y_off[:, 0:128]
            y_parts[2 * g + 1] = y_parts[2 * g + 1] + y_off[:, 128:256]
            upd = lax.dot_general(bm, x_dec[:, gcols], (((0,), (0,)), ((), ())),
                                  preferred_element_type=F32)
            state_ref[:, gcols] = st * chunk_decay[:, gcols] + upd

        y = jnp.concatenate(y_parts, axis=1) + xs * d_skip_ref[...]
        y = y * _silu(proj_ref[0, rows, C_Z:C_Z + W_GRP])
        outs = []
        for g in range(SSM_GROUPS):
            yg = y[:, g * 256:(g + 1) * 256]
            outs.append(yg * lax.rsqrt(jnp.mean(yg * yg, axis=-1, keepdims=True) + EPS))
        yd_out = jnp.concatenate(outs, axis=1) * norm_g_ref[...]
        y_ref[0, rows, 3 * W_GRP:4 * W_GRP] = yd_out.astype(BF16)


def _mixer(proj, mp, *, ts=256):
    b, s, _ = proj.shape
    small = [mp["pool_w"], mp["pool_scale"], mp["dw_w"], mp["dw_b"], mp["cln_g"], mp["cln_b"],
             mp["pw_w"], mp["pw_b"], mp["sln_g"], mp["sln_b"], mp["sgu_w"], mp["sgu_bias"],
             mp["sconv_w"], mp["sconv_b"], mp["dt_bias"], mp["a_log"], mp["d_skip"], mp["norm_g"],
             mp["head_sel"]]

    def const_spec(a):
        nd = a.ndim
        return pl.BlockSpec(a.shape, lambda bi, ti, _nd=nd: (0,) * _nd)

    return pl.pallas_call(
        functools.partial(_mixer_kernel, ts=ts),
        grid=(b, s // ts),
        in_specs=[pl.BlockSpec((1, ts, PROJ_COLS), lambda bi, ti: (bi, ti, 0))]
                 + [const_spec(a) for a in small],
        out_specs=pl.BlockSpec((1, ts, D_MODEL), lambda bi, ti: (bi, ti, 0)),
        out_shape=jax.ShapeDtypeStruct((b, s, D_MODEL), BF16),
        scratch_shapes=[
            pltpu.VMEM((HALO + ts, W_GRP), F32),
            pltpu.VMEM((HALO + ts, W_GRP), F32),
            pltpu.VMEM((HALO + ts, SSM_CONV_DIM), F32),
            pltpu.VMEM((SSM_STATE, W_GRP), F32),
        ],
        compiler_params=_params(("parallel", "arbitrary")),
        name="mixer",
    )(proj, *small)


def _expand_heads(v):
    return jnp.repeat(v, SSM_HEAD_DIM)[None, :]


def kernel(x, ffn1_norm, ffn1_w_gate, ffn1_w_up, ffn1_w_down, mix_norm, w_in, pool_w, pool_scale, conv_dw_w, conv_dw_b, conv_ln_g, conv_ln_b, conv_pw_w, conv_pw_b, sgu_ln_g, sgu_ln_b, sgu_w_s, sgu_b, ssm_conv_w, ssm_conv_b, ssm_dt_bias, ssm_a_log, ssm_d, ssm_norm, w_out, ffn2_norm, ffn2_w_gate, ffn2_w_up, ffn2_w_down, final_norm):
    bsz, seq, _ = x.shape
    m = bsz * seq
    xf = x.reshape(m, D_MODEL)
    head_sel = (jnp.arange(W_GRP)[None, :] == (jnp.arange(SSM_HEADS) * SSM_HEAD_DIM)[:, None]).astype(F32)

    for l in range(DEPTH):
        xf = _ffn(xf, ffn1_norm[l][None, :], ffn1_w_gate[l].astype(BF16), ffn1_w_up[l].astype(BF16),
                  ffn1_w_down[l].astype(BF16))

        w_in_aug = jnp.concatenate(
            [w_in[l][:, :N_MAIN], jnp.repeat(w_in[l][:, N_MAIN:], SSM_HEAD_DIM, axis=1)], axis=1).astype(BF16)
        proj = _proj(xf, mix_norm[l][None, :], w_in_aug).reshape(bsz, seq, PROJ_COLS)
        mp = dict(
            pool_w=pool_w[l].astype(BF16), pool_scale=pool_scale[l][None, :],
            dw_w=conv_dw_w[l], dw_b=conv_dw_b[l][None, :], cln_g=conv_ln_g[l][None, :],
            cln_b=conv_ln_b[l][None, :], pw_w=conv_pw_w[l].astype(BF16), pw_b=conv_pw_b[l][None, :],
            sln_g=sgu_ln_g[l][None, :], sln_b=sgu_ln_b[l][None, :], sgu_w=sgu_w_s[l],
            sgu_bias=jnp.repeat(sgu_b[l].T, W_GRP // SGU_HEADS, axis=1),
            sconv_w=ssm_conv_w[l], sconv_b=ssm_conv_b[l][None, :],
            dt_bias=_expand_heads(ssm_dt_bias[l]), a_log=_expand_heads(ssm_a_log[l]),
            d_skip=_expand_heads(ssm_d[l]), norm_g=ssm_norm[l][None, :], head_sel=head_sel,
        )
        y = _mixer(proj, mp).reshape(m, D_MODEL)
        xf = _outproj(y, w_out[l].astype(BF16), xf)

        xf = _ffn(xf, ffn2_norm[l][None, :], ffn2_w_gate[l].astype(BF16), ffn2_w_up[l].astype(BF16),
                  ffn2_w_down[l].astype(BF16))

    return _final_norm(xf, final_norm[None, :]).reshape(bsz, seq, D_MODEL)
```

```python
import functools

import jax
import jax.numpy as jnp
from jax import lax
from jax.experimental import pallas as pl
from jax.experimental.pallas import tpu as pltpu

F32 = jnp.float32
BF16 = jnp.bfloat16

D_MODEL = 2048
DEPTH = 4
D_FF = 5632
EPS = 1e-6

W_GRP = 512
POOL_WINDOWS = (2, 4, 8, 16)
POOL_GROUP = 128
CONV_WIDTH = 31
CHUNK = 128
SGU_HEADS = 4
SSM_HEADS = 8
SSM_HEAD_DIM = 64
SSM_GROUPS = 2
SSM_STATE = 128
SSM_CONV = 4
SSM_CONV_DIM = 1024
N_MAIN = 4096
PROJ_COLS = N_MAIN + W_GRP
HALO = 32
SUBLANES = 8
CONV_BACK = (CONV_WIDTH - 1) // SUBLANES * SUBLANES
COL_STEP = 256

C_A, C_VAL, C_GATE, C_U, C_V, C_Z, C_XBC, C_DT = 0, 512, 1024, 1536, 2048, 2560, 3072, 4096

VMEM_LIMIT = 60 * 1024 * 1024


def _params(sem):
    return pltpu.CompilerParams(dimension_semantics=sem, vmem_limit_bytes=VMEM_LIMIT)


def _rms_rows(x, g):
    ms = jnp.mean(x * x, axis=-1, keepdims=True)
    return x * lax.rsqrt(ms + EPS) * g


def _silu(x):
    return x * jax.nn.sigmoid(x)


def _gelu_tanh(x):
    c = 0.7978845608028654
    return 0.5 * x * (1.0 + jnp.tanh(c * (x + 0.044715 * (x * x * x))))


def _layernorm_rows(x, g, b):
    mu = jnp.mean(x, axis=-1, keepdims=True)
    xc = x - mu
    var = jnp.mean(xc * xc, axis=-1, keepdims=True)
    return xc * lax.rsqrt(var + EPS) * g + b


def _ffn_kernel(layer_ref, *refs, n_f, final_norm):
    del layer_ref
    if final_norm:
        x_ref, g_ref, wg_ref, wu_ref, wd_ref, gf_ref, o_ref, h_ref = refs
    else:
        x_ref, g_ref, wg_ref, wu_ref, wd_ref, o_ref, h_ref = refs
    j = pl.program_id(1)

    @pl.when(j == 0)
    def _():
        x = x_ref[...]
        h_ref[...] = _rms_rows(x, g_ref[...]).astype(BF16)
        o_ref[...] = x

    h = h_ref[...]
    gate = jnp.dot(h, wg_ref[...], preferred_element_type=F32)
    up = jnp.dot(h, wu_ref[...], preferred_element_type=F32)
    act = (_silu(gate) * (0.5 * up)).astype(BF16)
    o_ref[...] += jnp.dot(act, wd_ref[...], preferred_element_type=F32)

    if final_norm:
        @pl.when(j == n_f - 1)
        def _():
            o_ref[...] = _rms_rows(o_ref[...], gf_ref[...])


def _ffn(x, g, wg, wu, wd, layer, final_g=None, *, tm=1024, tf=512):
    m = x.shape[0]
    tm = min(tm, m)
    n_f = D_FF // tf
    final_norm = final_g is not None
    in_specs = [
        pl.BlockSpec((tm, D_MODEL), lambda i, j, l: (i, 0)),
        pl.BlockSpec((None, 1, D_MODEL), lambda i, j, l: (l[0], 0, 0)),
        pl.BlockSpec((None, D_MODEL, tf), lambda i, j, l: (l[0], 0, j)),
        pl.BlockSpec((None, D_MODEL, tf), lambda i, j, l: (l[0], 0, j)),
        pl.BlockSpec((None, tf, D_MODEL), lambda i, j, l: (l[0], j, 0)),
    ]
    args = [x, g, wg, wu, wd]
    if final_norm:
        in_specs.append(pl.BlockSpec((1, D_MODEL), lambda i, j, l: (0, 0)))
        args.append(final_g)
    return pl.pallas_call(
        functools.partial(_ffn_kernel, n_f=n_f, final_norm=final_norm),
        grid_spec=pltpu.PrefetchScalarGridSpec(
            num_scalar_prefetch=1,
            grid=(m // tm, n_f),
            in_specs=in_specs,
            out_specs=pl.BlockSpec((tm, D_MODEL), lambda i, j, l: (i, 0)),
            scratch_shapes=[pltpu.VMEM((tm, D_MODEL), BF16)],
        ),
        out_shape=jax.ShapeDtypeStruct((m, D_MODEL), F32),
        compiler_params=_params(("parallel", "arbitrary")),
        name="ffn_final" if final_norm else "ffn",
    )(jnp.full((1,), layer, jnp.int32), *args)


def _mix_chunk(p_ref, y_ref, c, pos0, prm, consts, ext_a, ext_h, ext_xbc, shift_ref, state_ref, fill):
    (pool_w_ref, pool_scale_ref, dw_w_ref, dw_b_ref, cln_g_ref, cln_b_ref, pw_w_ref, pw_b_ref,
     sln_g_ref, sln_b_ref, sgu_bias_ref, sconv_w_ref, sconv_b_ref, dt_bias_ref, d_skip_ref,
     norm_g_ref, head_sel_ref) = prm
    row_i, causal, tril_ones, lane_lo, sgu_w, a_neg = consts
    rows = slice(c * CHUNK, (c + 1) * CHUNK)
    base = HALO + c * CHUNK

    pos1 = row_i + (pos0 + c * CHUNK + 1)
    for g, win in enumerate(POOL_WINDOWS):
        cols = slice(g * POOL_GROUP, (g + 1) * POOL_GROUP)
        cur = ext_a[base:base + CHUNK, cols]
        s = cur
        for k in range(1, win):
            s = s + ext_a[base - k:base - k + CHUNK, cols]
        cnt = jnp.minimum(pos1, win).astype(F32)
        p = s / cnt - cur
        ya = jnp.dot(p.astype(BF16), pool_w_ref[g], preferred_element_type=F32)
        y_ref[rows, cols] = (ya * pool_scale_ref[:, cols]).astype(BF16)
        if g >= 2:
            fill()

    blocks = []
    for cb in range(W_GRP // 128):
        cols = slice(cb * 128, (cb + 1) * 128)
        for r in range(1, SUBLANES):
            shift_ref[r - 1, :, cols] = ext_h[base - CONV_BACK - r:base + CHUNK - r, cols]
        acc = None
        for sh in range(CONV_WIDTH):
            q, r = divmod(sh, SUBLANES)
            j = CONV_WIDTH - 1 - sh
            lo = CONV_BACK - SUBLANES * q
            if r == 0:
                tap = ext_h[base - SUBLANES * q:base - SUBLANES * q + CHUNK, cols]
            else:
                tap = shift_ref[r - 1, lo:lo + CHUNK, cols]
            term = tap * dw_w_ref[j:j + 1, cols]
            acc = term if acc is None else acc + term
        blocks.append(acc)
        fill()
    hc = jnp.concatenate(blocks, axis=1) + dw_b_ref[...]
    hs = _silu(_layernorm_rows(hc, cln_g_ref[...], cln_b_ref[...]))
    yb = jnp.dot(hs.astype(BF16), pw_w_ref[...], preferred_element_type=F32) + pw_b_ref[...]
    y_ref[rows, W_GRP:2 * W_GRP] = yb.astype(BF16)
    fill()

    u = _gelu_tanh(p_ref[rows, C_U:C_U + W_GRP])
    v = _layernorm_rows(_gelu_tanh(p_ref[rows, C_V:C_V + W_GRP]),
                        sln_g_ref[...], sln_b_ref[...]).astype(BF16)
    fill()
    for h in range(SGU_HEADS):
        cols = slice(h * 128, (h + 1) * 128)
        gt = jnp.dot(sgu_w[h], v[:, cols], preferred_element_type=F32) + sgu_bias_ref[:, cols]
        y_ref[rows, 2 * W_GRP + h * 128:2 * W_GRP + (h + 1) * 128] = (u[:, cols] * gt).astype(BF16)

    xbc = ext_xbc[base:base + CHUNK, :] * sconv_w_ref[SSM_CONV - 1:SSM_CONV, :]
    for j in range(SSM_CONV - 1):
        sh = SSM_CONV - 1 - j
        xbc = xbc + ext_xbc[base - sh:base - sh + CHUNK, :] * sconv_w_ref[j:j + 1, :]
    xbc = _silu(xbc + sconv_b_ref[...])
    fill()
    xs = xbc[:, 0:W_GRP]
    dt_pre = p_ref[rows, C_DT:C_DT + W_GRP] + dt_bias_ref[...]
    dt = jnp.maximum(dt_pre, 0.0) + jnp.log1p(jnp.exp(-jnp.abs(dt_pre)))
    x_c = xs * dt
    a_cs = jnp.dot(tril_ones, dt * a_neg, preferred_element_type=F32,
                   precision=lax.Precision.HIGHEST)
    a_last = a_cs[CHUNK - 1:CHUNK, :]
    exp_acs = jnp.exp(a_cs)
    x_dec = (x_c * jnp.exp(a_last - a_cs)).astype(BF16)
    chunk_decay = jnp.exp(a_last)
    a_cs_t = lax.dot_general(head_sel_ref[...], a_cs, (((1,), (1,)), ((), ())),
                             preferred_element_type=F32, precision=lax.Precision.HIGHEST)
    fill()

    y_parts = []
    for g in range(SSM_GROUPS):
        bm = xbc[:, W_GRP + g * SSM_STATE:W_GRP + (g + 1) * SSM_STATE].astype(BF16)
        cm = xbc[:, W_GRP + (SSM_GROUPS + g) * SSM_STATE:
                 W_GRP + (SSM_GROUPS + g + 1) * SSM_STATE].astype(BF16)
        scores = lax.dot_general(cm, bm, (((1,), (1,)), ((), ())), preferred_element_type=F32)
        gcols = slice(g * 256, (g + 1) * 256)
        for jb in range(2):
            blk = 2 * g + jb
            cols = slice(blk * 128, (blk + 1) * 128)
            acs_blk = a_cs[:, cols]
            swapped = pltpu.roll(acs_blk, SSM_HEAD_DIM, axis=1)
            xcb = x_c[:, cols]
            yd = None
            for half in range(2):
                head = 2 * blk + half
                col_b = jnp.where(lane_lo, acs_blk, swapped) if half == 0 else \
                    jnp.where(lane_lo, swapped, acs_blk)
                seg = col_b - a_cs_t[head:head + 1, :]
                decay = jnp.where(causal, jnp.exp(seg), 0.0)
                keep = lane_lo if half == 0 else jnp.logical_not(lane_lo)
                xh = jnp.where(keep, xcb, 0.0).astype(BF16)
                part = jnp.dot((scores * decay).astype(BF16), xh, preferred_element_type=F32)
                yd = part if yd is None else yd + part
            y_parts.append(yd)
            fill()
        st = state_ref[:, gcols]
        y_off = jnp.dot(cm, st.astype(BF16), preferred_element_type=F32) * exp_acs[:, gcols]
        y_parts[2 * g] = y_parts[2 * g] + y_off[:, 0:128]
        y_parts[2 * g + 1] = y_parts[2 * g + 1] + y_off[:, 128:256]
        upd = lax.dot_general(bm, x_dec[:, gcols], (((0,), (0,)), ((), ())),
                              preferred_element_type=F32)
        state_ref[:, gcols] = st * chunk_decay[:, gcols] + upd

    y = jnp.concatenate(y_parts, axis=1) + xs * d_skip_ref[...]
    y = y * _silu(p_ref[rows, C_Z:C_Z + W_GRP])
    outs = []
    for g in range(SSM_GROUPS):
        yg = y[:, g * 256:(g + 1) * 256]
        outs.append(yg * lax.rsqrt(jnp.mean(yg * yg, axis=-1, keepdims=True) + EPS))
    yd_out = jnp.concatenate(outs, axis=1) * norm_g_ref[...]
    y_ref[rows, 3 * W_GRP:4 * W_GRP] = yd_out.astype(BF16)
    fill()


def _mix_layer_kernel(layer_ref, x_in_ref, x_res_ref, g_ref, w_in_ref, w_out_ref,
                      pool_w_ref, pool_scale_ref, dw_w_ref, dw_b_ref, cln_g_ref, cln_b_ref,
                      pw_w_ref, pw_b_ref, sln_g_ref, sln_b_ref, sgu_w_ref, sgu_bias_ref,
                      sconv_w_ref, sconv_b_ref, dt_bias_ref, a_log_ref, d_skip_ref, norm_g_ref,
                      head_sel_ref,
                      o_ref,
                      p0, p1, y0, y1, h_ref, ext_a, ext_h, ext_xbc, shift_ref, state_ref, *, ts, tiles_per_seq):
    del layer_ref
    s = pl.program_id(0)
    n_chunks = ts // CHUNK

    @pl.when(s == 0)
    def _():
        p1[...] = jnp.zeros_like(p1)
        y0[...] = jnp.zeros_like(y0)
        ext_a[...] = jnp.zeros_like(ext_a)
        ext_h[...] = jnp.zeros_like(ext_h)
        ext_xbc[...] = jnp.zeros_like(ext_xbc)
        state_ref[...] = jnp.zeros_like(state_ref)

    prm = (pool_w_ref, pool_scale_ref, dw_w_ref, dw_b_ref, cln_g_ref, cln_b_ref, pw_w_ref, pw_b_ref,
           sln_g_ref, sln_b_ref, sgu_bias_ref, sconv_w_ref, sconv_b_ref, dt_bias_ref, d_skip_ref,
           norm_g_ref, head_sel_ref)

    def body(p_w, p_r, y_w, y_r):
        def out_piece(n):
            cols = slice(n * COL_STEP, (n + 1) * COL_STEP)
            o_ref[:, cols] = x_res_ref[:, cols] + jnp.dot(y_r[...], w_out_ref[:, cols],
                                                          preferred_element_type=F32)

        def in_piece(n):
            cols = slice(n * COL_STEP, (n + 1) * COL_STEP)
            p_w[:, cols] = jnp.dot(h_ref[...], w_in_ref[:, cols], preferred_element_type=F32)

        pieces = iter([functools.partial(out_piece, n) for n in range(D_MODEL // COL_STEP)]
                      + [functools.partial(in_piece, n) for n in range(PROJ_COLS // COL_STEP)])

        def fill():
            piece = next(pieces, None)
            if piece is not None:
                piece()

        h_ref[...] = _rms_rows(x_in_ref[...], g_ref[...]).astype(BF16)

        t_seq = lax.rem(s + (tiles_per_seq - 1), tiles_per_seq)
        keep = jnp.where(t_seq == 0, 0.0, 1.0).astype(F32)
        ext_a[0:HALO, :] = ext_a[ts:ts + HALO, :] * keep
        ext_h[0:HALO, :] = ext_h[ts:ts + HALO, :] * keep
        ext_xbc[0:HALO, :] = ext_xbc[ts:ts + HALO, :] * keep
        state_ref[...] = state_ref[...] * keep
        for c in range(n_chunks):
            rows = slice(c * CHUNK, (c + 1) * CHUNK)
            dst = slice(HALO + c * CHUNK, HALO + (c + 1) * CHUNK)
            ext_a[dst, :] = p_r[rows, C_A:C_A + W_GRP]
            ext_h[dst, :] = p_r[rows, C_VAL:C_VAL + W_GRP] * jax.nn.sigmoid(p_r[rows, C_GATE:C_GATE + W_GRP])
            ext_xbc[dst, :] = p_r[rows, C_XBC:C_XBC + SSM_CONV_DIM]
            fill()

        row_i = lax.broadcasted_iota(jnp.int32, (CHUNK, CHUNK), 0)
        col_i = lax.broadcasted_iota(jnp.int32, (CHUNK, CHUNK), 1)
        causal = row_i >= col_i
        tril_ones = jnp.where(causal, 1.0, 0.0).astype(F32)
        lane_lo = col_i < SSM_HEAD_DIM
        sgu_w = [jnp.where(causal, sgu_w_ref[hh], 0.0).astype(BF16) for hh in range(SGU_HEADS)]
        a_neg = -jnp.exp(a_log_ref[...])
        consts = (row_i, causal, tril_ones, lane_lo, sgu_w, a_neg)
        for c in range(n_chunks):
            _mix_chunk(p_r, y_w, c, t_seq * ts, prm, consts, ext_a, ext_h, ext_xbc, shift_ref, state_ref, fill)

        for piece in pieces:
            piece()

    @pl.when(lax.rem(s, 2) == 0)
    def _():
        body(p0, p1, y1, y0)

    @pl.when(lax.rem(s, 2) == 1)
    def _():
        body(p1, p0, y0, y1)


def _mix_layer(x, g, w_in_aug, w_out, small, layer, seq, *, ts=256):
    m = x.shape[0]
    n_tiles = m // ts
    tiles_per_seq = seq // ts

    def layer_spec(a, single=False):
        nd = a.ndim
        kw = dict(pipeline_mode=pl.Buffered(1)) if single else {}
        return pl.BlockSpec((None,) + a.shape[1:], lambda s, l, _nd=nd: (l[0],) + (0,) * (_nd - 1), **kw)

    def const_spec(a):
        nd = a.ndim
        return pl.BlockSpec(a.shape, lambda s, l, _nd=nd: (0,) * _nd)

    in_specs = [
        pl.BlockSpec((ts, D_MODEL), lambda s, l: (jnp.minimum(s, n_tiles - 1), 0)),
        pl.BlockSpec((ts, D_MODEL), lambda s, l: (jnp.clip(s - 2, 0, n_tiles - 1), 0)),
        pl.BlockSpec((None, 1, D_MODEL), lambda s, l: (l[0], 0, 0)),
        layer_spec(w_in_aug, single=True),
        layer_spec(w_out, single=True),
    ] + [layer_spec(a) for a in small[:-1]] + [const_spec(small[-1])]

    return pl.pallas_call(
        functools.partial(_mix_layer_kernel, ts=ts, tiles_per_seq=tiles_per_seq),
        grid_spec=pltpu.PrefetchScalarGridSpec(
            num_scalar_prefetch=1,
            grid=(n_tiles + 2,),
            in_specs=in_specs,
            out_specs=pl.BlockSpec((ts, D_MODEL), lambda s, l: (jnp.clip(s - 2, 0, n_tiles - 1), 0)),
            scratch_shapes=[
                pltpu.VMEM((ts, PROJ_COLS), F32),
                pltpu.VMEM((ts, PROJ_COLS), F32),
                pltpu.VMEM((ts, D_MODEL), BF16),
                pltpu.VMEM((ts, D_MODEL), BF16),
                pltpu.VMEM((ts, D_MODEL), BF16),
                pltpu.VMEM((HALO + ts, W_GRP), F32),
                pltpu.VMEM((HALO + ts, W_GRP), F32),
                pltpu.VMEM((HALO + ts, SSM_CONV_DIM), F32),
                pltpu.VMEM((SUBLANES - 1, CONV_BACK + CHUNK, W_GRP), F32),
                pltpu.VMEM((SSM_STATE, W_GRP), F32),
            ],
        ),
        out_shape=jax.ShapeDtypeStruct((m, D_MODEL), F32),
        compiler_params=_params(("arbitrary",)),
        name="mix_layer",
    )(jnp.full((1,), layer, jnp.int32), x, x, g, w_in_aug, w_out, *small)


def _expand_heads(v):
    return jnp.repeat(v, SSM_HEAD_DIM, axis=-1)[:, None, :]


def kernel(x, ffn1_norm, ffn1_w_gate, ffn1_w_up, ffn1_w_down, mix_norm, w_in, pool_w, pool_scale, conv_dw_w, conv_dw_b, conv_ln_g, conv_ln_b, conv_pw_w, conv_pw_b, sgu_ln_g, sgu_ln_b, sgu_w_s, sgu_b, ssm_conv_w, ssm_conv_b, ssm_dt_bias, ssm_a_log, ssm_d, ssm_norm, w_out, ffn2_norm, ffn2_w_gate, ffn2_w_up, ffn2_w_down, final_norm):
    bsz, seq, _ = x.shape
    m = bsz * seq
    depth = ffn1_norm.shape[0]
    xf = x.reshape(m, D_MODEL)

    f1 = [w.astype(BF16) for w in (ffn1_w_gate, ffn1_w_up, ffn1_w_down)]
    f2 = [w.astype(BF16) for w in (ffn2_w_gate, ffn2_w_up, ffn2_w_down)]
    w_in_aug = jnp.concatenate(
        [w_in[:, :, :N_MAIN], jnp.repeat(w_in[:, :, N_MAIN:], SSM_HEAD_DIM, axis=2)], axis=2).astype(BF16)
    w_out_b = w_out.astype(BF16)
    head_sel = (jnp.arange(W_GRP)[None, :] == (jnp.arange(SSM_HEADS) * SSM_HEAD_DIM)[:, None]).astype(F32)
    row = lambda a: a[:, None, :]
    small = [
        pool_w.astype(BF16), row(pool_scale), conv_dw_w, row(conv_dw_b), row(conv_ln_g), row(conv_ln_b),
        conv_pw_w.astype(BF16), row(conv_pw_b), row(sgu_ln_g), row(sgu_ln_b), sgu_w_s,
        jnp.repeat(jnp.swapaxes(sgu_b, 1, 2), W_GRP // SGU_HEADS, axis=2),
        ssm_conv_w, row(ssm_conv_b), _expand_heads(ssm_dt_bias), _expand_heads(ssm_a_log),
        _expand_heads(ssm_d), row(ssm_norm), head_sel,
    ]

    for l in range(depth):
        xf = _ffn(xf, row(ffn1_norm), *f1, l)
        xf = _mix_layer(xf, row(mix_norm), w_in_aug, w_out_b, small, l, seq)
        xf = _ffn(xf, row(ffn2_norm), *f2, l, final_g=final_norm[None, :] if l == depth - 1 else None)

    return xf.reshape(bsz, seq, D_MODEL)
```
